```python
import functools
import jax, jax.numpy as jnp
from jax import lax
import numpy as np

D_MODEL = 2048
BATCH = 2
SEQ = 4096
DEPTH = 4
DEC_BATCH = 32
DEC_SEQ = 8
PAST_LEN = 16384
PAGE_SIZE = 128

N_META = 16
N_HEADS = 16
N_KV_HEADS = 4
HEAD_DIM = 64
GROUP = N_HEADS // N_KV_HEADS
WINDOW = 128
BLOCK = 128
ROPE_DIM = HEAD_DIM // 4
ROPE_THETA = 500000.0
CONV_CH = 1024
CONV_W = 31
D_FF = -(-8 * D_MODEL // (3 * 256)) * 256
Q_W = N_HEADS * HEAD_DIM
KV_W = N_KV_HEADS * HEAD_DIM
IN_COLS = Q_W + 2 * KV_W + 2 * CONV_CH + 2 * D_MODEL
EPS = 1e-6
NEG = -1e30

kernel_name = "gated_swa_conformer_hybrid_step"


def rms_norm(x, g):
    xf = x.astype(jnp.float32)
    y = xf * lax.rsqrt(jnp.mean(xf * xf, axis=-1, keepdims=True) + EPS)
    return (y * g.astype(jnp.float32)).astype(x.dtype)


def layer_norm(x, g, b):
    xf = x.astype(jnp.float32)
    mu = jnp.mean(xf, axis=-1, keepdims=True)
    var = jnp.mean(jnp.square(xf - mu), axis=-1, keepdims=True)
    y = (xf - mu) * lax.rsqrt(var + EPS)
    return (y * g.astype(jnp.float32) + b.astype(jnp.float32)).astype(x.dtype)


def rotary(x, pos):
    half = ROPE_DIM // 2
    inv = ROPE_THETA ** (-jnp.arange(half, dtype=jnp.float32) * 2.0 / ROPE_DIM)
    ang = pos.astype(jnp.float32)[:, None] * inv[None, :]
    cos = jnp.cos(ang)[:, None, :].astype(x.dtype)
    sin = jnp.sin(ang)[:, None, :].astype(x.dtype)
    x1 = x[..., :half]
    x2 = x[..., half:ROPE_DIM]
    return jnp.concatenate([x1 * cos - x2 * sin, x2 * cos + x1 * sin, x[..., ROPE_DIM:]], axis=-1)


def sink_attention(q, k, v, q_pos, k_pos, sinks):
    s = jnp.einsum('bnqhgd,bnkhd->bnhgqk', q, k).astype(jnp.float32) * (HEAD_DIM ** -0.5)
    diff = q_pos[:, :, None] - k_pos[:, None, :]
    mask = (diff >= 0) & (diff <= WINDOW) & (k_pos[:, None, :] >= 0)
    s = jnp.where(mask[None, :, None, None], s, NEG)
    sink = jnp.broadcast_to(sinks.astype(jnp.float32).reshape(N_KV_HEADS, GROUP)[None, None, :, :, None, None],
                            s.shape[:-1] + (1,))
    p = jax.nn.softmax(jnp.concatenate([s, sink], axis=-1), axis=-1)[..., :-1]
    return jnp.einsum('bnhgqk,bnkhd->bnqhgd', p.astype(v.dtype), v)


def attend_prompt(q, k, v, sinks):
    B, L = q.shape[0], q.shape[1]
    pad = (-N_META) % BLOCK
    nb = (L + pad) // BLOCK
    padf = lambda a: jnp.pad(a, ((0, 0), (pad, 0)) + ((0, 0),) * (a.ndim - 2))
    shift = lambda a: jnp.concatenate([jnp.zeros_like(a[:, :1]), a[:, :-1]], axis=1)
    qb = padf(q).reshape(B, nb, BLOCK, N_KV_HEADS, GROUP, HEAD_DIM)
    kb = padf(k).reshape(B, nb, BLOCK, N_KV_HEADS, HEAD_DIM)
    vb = padf(v).reshape(B, nb, BLOCK, N_KV_HEADS, HEAD_DIM)
    k2 = jnp.concatenate([shift(kb), kb], axis=2)
    v2 = jnp.concatenate([shift(vb), vb], axis=2)
    pos = jnp.concatenate([jnp.full((pad,), -1, jnp.int32), jnp.arange(L, dtype=jnp.int32)]).reshape(nb, BLOCK)
    pos_prev = jnp.concatenate([jnp.full((1, BLOCK), -1, jnp.int32), pos[:-1]], axis=0)
    kpos = jnp.concatenate([pos_prev, pos], axis=1)
    o = sink_attention(qb, k2, v2, pos, kpos, sinks)
    o = o.reshape(B, nb * BLOCK, Q_W)[:, pad:]
    return o, k[:, -WINDOW:], v[:, -WINDOW:]


def attend_sample(q, k, v, sinks, past_k, past_v):
    DB, T = q.shape[0], q.shape[1]
    kk = jnp.concatenate([past_k, k], axis=1)
    vv = jnp.concatenate([past_v, v], axis=1)
    qpos = PAST_LEN + jnp.arange(T, dtype=jnp.int32)
    kpos = jnp.concatenate([PAST_LEN - WINDOW + jnp.arange(WINDOW, dtype=jnp.int32), qpos])
    o = sink_attention(q.reshape(DB, 1, T, N_KV_HEADS, GROUP, HEAD_DIM), kk[:, None], vv[:, None],
                       qpos[None], kpos[None], sinks)
    return o.reshape(DB, T, Q_W), kk[:, -WINDOW:], vv[:, -WINDOW:]


def conv_module(a_glu, past, w_dw, b_dw, ln_g, ln_b, w_out, b_out):
    u = a_glu[..., :CONV_CH] * jax.nn.sigmoid(a_glu[..., CONV_CH:])
    ext = jnp.concatenate([past.astype(u.dtype), u], axis=1)
    y = lax.conv_general_dilated(ext, w_dw[:, None, :].astype(u.dtype), window_strides=(1,), padding='VALID',
                                 dimension_numbers=('NWC', 'WIO', 'NWC'),
                                 feature_group_count=CONV_CH) + b_dw
    y = jax.nn.silu(layer_norm(y, ln_g, ln_b))
    return y @ w_out + b_out, ext[:, -(CONV_W - 1):]


def trunk_layer(x, pos, attend, conv_past, w_in, w_attn_out, w_dw, b_dw, ln_g, ln_b,
                w_conv_out, b_conv_out, w_o, g_mix, g_ffn, w_ffn_in, w_ffn_out):
    B, T = x.shape[0], x.shape[1]
    h = rms_norm(x, g_mix)
    u = h @ w_in
    o1 = Q_W
    o2 = o1 + KV_W
    o3 = o2 + KV_W
    o4 = o3 + 2 * CONV_CH
    o5 = o4 + D_MODEL
    q = rotary(u[..., :o1].reshape(B, T, N_HEADS, HEAD_DIM), pos)
    k = rotary(u[..., o1:o2].reshape(B, T, N_KV_HEADS, HEAD_DIM), pos)
    v = u[..., o2:o3].reshape(B, T, N_KV_HEADS, HEAD_DIM)
    a_glu = u[..., o3:o4]
    g_a = u[..., o4:o5]
    g_b = u[..., o5:]
    o_attn, new_k, new_v = attend(q, k, v)
    br_a = o_attn @ w_attn_out
    br_b, new_conv = conv_module(a_glu, conv_past, w_dw, b_dw, ln_g, ln_b, w_conv_out, b_conv_out)
    x = x + (jax.nn.sigmoid(g_a) * br_a + jax.nn.sigmoid(g_b) * br_b) @ w_o
    h = rms_norm(x, g_ffn)
    gu = h @ w_ffn_in
    x = x + (jax.nn.silu(gu[..., :D_FF]) * gu[..., D_FF:]) @ w_ffn_out
    return x, new_k, new_v, new_conv


def setup_inputs(seed: int = 0) -> dict:
    key = jax.random.key(seed)
    ks = jax.random.split(key, 24)
    nrm = lambda k, shape, s: jax.random.normal(k, shape, jnp.float32) * s
    return {
        "x_prompt": nrm(ks[0], (BATCH, SEQ, D_MODEL), 1.0),
        "x_sample": nrm(ks[1], (DEC_BATCH, DEC_SEQ, D_MODEL), 1.0),
        "cache_k": nrm(ks[2], (DEPTH, DEC_BATCH, WINDOW, N_KV_HEADS, HEAD_DIM), 1.0),
        "cache_v": nrm(ks[3], (DEPTH, DEC_BATCH, WINDOW, N_KV_HEADS, HEAD_DIM), 1.0),
        "state_conv": nrm(ks[4], (DEPTH, DEC_BATCH, CONV_W - 1, CONV_CH), 0.5),
        "meta_tokens": nrm(ks[5], (N_META, D_MODEL), 1.0),
        "w_in": nrm(ks[6], (DEPTH, D_MODEL, IN_COLS), D_MODEL ** -0.5),
        "attn_sinks": nrm(ks[7], (DEPTH, N_HEADS), 0.5),
        "w_attn_out": nrm(ks[8], (DEPTH, Q_W, D_MODEL), Q_W ** -0.5),
        "conv_dw": nrm(ks[9], (DEPTH, CONV_W, CONV_CH), CONV_W ** -0.5),
        "conv_dw_bias": nrm(ks[10], (DEPTH, CONV_CH), 0.02),
        "conv_ln_g": 1.0 + nrm(ks[11], (DEPTH, CONV_CH), 0.02),
        "conv_ln_b": nrm(ks[12], (DEPTH, CONV_CH), 0.02),
        "w_conv_out": nrm(ks[13], (DEPTH, CONV_CH, D_MODEL), CONV_CH ** -0.5),
        "b_conv_out": nrm(ks[14], (DEPTH, D_MODEL), 0.02),
        "w_o": nrm(ks[15], (DEPTH, D_MODEL, D_MODEL), D_MODEL ** -0.5),
        "norm_mix": 1.0 + nrm(ks[16], (DEPTH, D_MODEL), 0.02),
        "norm_ffn": 1.0 + nrm(ks[17], (DEPTH, D_MODEL), 0.02),
        "w_ffn_in": nrm(ks[18], (DEPTH, D_MODEL, 2 * D_FF), D_MODEL ** -0.5),
        "w_ffn_out": nrm(ks[19], (DEPTH, D_FF, D_MODEL), D_FF ** -0.5),
        "norm_final": 1.0 + nrm(ks[20], (D_MODEL,), 0.02),
    }


def reference(x_prompt, x_sample, cache_k, cache_v, state_conv, meta_tokens, w_in, attn_sinks, w_attn_out,
              conv_dw, conv_dw_bias, conv_ln_g, conv_ln_b, w_conv_out, b_conv_out, w_o, norm_mix, norm_ffn,
              w_ffn_in, w_ffn_out, norm_final):
    B = x_prompt.shape[0]
    DB, T = x_sample.shape[0], x_sample.shape[1]
    meta = jnp.broadcast_to(meta_tokens.astype(x_prompt.dtype)[None], (B, N_META, D_MODEL))
    xp = jnp.concatenate([meta, x_prompt], axis=1)
    L = xp.shape[1]
    pos_p = jnp.arange(L, dtype=jnp.int32)
    pos_s = PAST_LEN + jnp.arange(T, dtype=jnp.int32)
    xs = x_sample
    conv_zero = jnp.zeros((B, CONV_W - 1, CONV_CH), x_prompt.dtype)
    nk_p, nv_p, nc_p, nk_s, nv_s, nc_s = [], [], [], [], [], []
    for l in range(DEPTH):
        w = (w_in[l], w_attn_out[l], conv_dw[l], conv_dw_bias[l], conv_ln_g[l], conv_ln_b[l],
             w_conv_out[l], b_conv_out[l], w_o[l], norm_mix[l], norm_ffn[l], w_ffn_in[l], w_ffn_out[l])
        att_p = functools.partial(attend_prompt, sinks=attn_sinks[l])
        att_s = functools.partial(attend_sample, sinks=attn_sinks[l], past_k=cache_k[l], past_v=cache_v[l])
        xp, kp, vp, cp = trunk_layer(xp, pos_p, att_p, conv_zero, *w)
        xs, ks_, vs_, cs_ = trunk_layer(xs, pos_s, att_s, state_conv[l], *w)
        nk_p.append(kp)
        nv_p.append(vp)
        nc_p.append(cp)
        nk_s.append(ks_)
        nv_s.append(vs_)
        nc_s.append(cs_)
    y_prompt = rms_norm(xp, norm_final)[:, N_META:]
    y_sample = rms_norm(xs, norm_final)
    new_k_prompt = jnp.stack(nk_p)
    new_v_prompt = jnp.stack(nv_p)
    new_conv_prompt = jnp.stack(nc_p)
    new_k_sample = jnp.stack(nk_s)
    new_v_sample = jnp.stack(nv_s)
    new_conv_sample = jnp.stack(nc_s)
    return (y_prompt, y_sample, new_k_prompt, new_v_prompt, new_conv_prompt, new_k_sample, new_v_sample, new_conv_sample)
```

```python
import functools

import jax
import jax.numpy as jnp
from jax import lax
from jax.experimental import pallas as pl
from jax.experimental.pallas import tpu as pltpu

F32 = jnp.float32
BF16 = jnp.bfloat16

D_MODEL = 2048
DEPTH = 4
BATCH = 2
SEQ = 4096
DEC_BATCH = 32
DEC_SEQ = 8
PAST_LEN = 16384
N_META = 16
N_HEADS = 16
N_KV_HEADS = 4
HEAD_DIM = 64
GROUP = N_HEADS // N_KV_HEADS
WINDOW = 128
ROPE_DIM = 16
ROPE_HALF = ROPE_DIM // 2
ROPE_THETA = 500000.0
CONV_CH = 1024
CONV_W = 31
D_FF = 5632
Q_W = N_HEADS * HEAD_DIM
KV_W = N_KV_HEADS * HEAD_DIM
QKV_W = Q_W + 2 * KV_W
EPS = 1e-6
NEG = -1e30

LANES = 128
BLK = 128
L_REAL = N_META + SEQ
L_PAD = 4224
NB = L_PAD // BLK
N_PROMPT = BATCH * L_PAD
N_SAMPLE = DEC_BATCH * DEC_SEQ
N_ROWS = N_PROMPT + N_SAMPLE
CONV_HIST = 32
CONV_OFF = CONV_HIST - (CONV_W - 1)
SEQ_PER_STEP = 8

TM = 512
TM_SMALL = 256
VMEM_LIMIT = 56 * 1024 * 1024


def _params(n_axes, vmem=VMEM_LIMIT):
    return pltpu.CompilerParams(dimension_semantics=("arbitrary",) * n_axes, vmem_limit_bytes=vmem)


def _resident(shape, index_map):
    return pl.BlockSpec(shape, index_map, pipeline_mode=pl.Buffered(1))


def _rms(x, g):
    return x * lax.rsqrt(jnp.mean(x * x, axis=-1, keepdims=True) + EPS) * g


def _rmsnorm_kernel(x_ref, g_ref, h_ref):
    h_ref[...] = _rms(x_ref[...], g_ref[...]).astype(h_ref.dtype)


def _rmsnorm(x, g):
    return pl.pallas_call(
        _rmsnorm_kernel,
        grid=(N_ROWS // TM,),
        in_specs=[pl.BlockSpec((TM, D_MODEL), lambda i: (i, 0)),
                  pl.BlockSpec((1, D_MODEL), lambda i: (0, 0))],
        out_specs=pl.BlockSpec((TM, D_MODEL), lambda i: (i, 0)),
        out_shape=jax.ShapeDtypeStruct((N_ROWS, D_MODEL), BF16),
        compiler_params=_params(1),
        name="rmsnorm",
    )(x, g)


def _qkv_kernel(h_ref, w_ref, c_ref, s1_ref, s2_ref, q_ref, k_ref, v_ref):
    u = jnp.dot(h_ref[...], w_ref[...], preferred_element_type=F32)
    cos = c_ref[...]
    s_up = s1_ref[...]
    s_dn = s2_ref[...]
    n_rot = (Q_W + KV_W) // LANES
    for j in range(n_rot):
        xc = u[:, j * LANES:(j + 1) * LANES]
        r = (xc * cos + pltpu.roll(xc, LANES - ROPE_HALF, 1) * s_up + pltpu.roll(xc, ROPE_HALF, 1) * s_dn)
        if j < Q_W // LANES:
            q_ref[:, j * LANES:(j + 1) * LANES] = (r * (HEAD_DIM ** -0.5)).astype(q_ref.dtype)
        else:
            jj = j - Q_W // LANES
            k_ref[:, jj * LANES:(jj + 1) * LANES] = r
    v_ref[...] = u[:, Q_W + KV_W:]


def _qkv(h, w_qkv, cos, s_up, s_dn):
    row = lambda i: (i, 0)
    return pl.pallas_call(
        _qkv_kernel,
        grid=(N_ROWS // TM,),
        in_specs=[pl.BlockSpec((TM, D_MODEL), row),
                  _resident((D_MODEL, QKV_W), lambda i: (0, 0)),
                  pl.BlockSpec((TM, LANES), row),
                  pl.BlockSpec((TM, LANES), row),
                  pl.BlockSpec((TM, LANES), row)],
        out_specs=[pl.BlockSpec((TM, Q_W), row),
                   pl.BlockSpec((TM, KV_W), row),
                   pl.BlockSpec((TM, KV_W), row)],
        out_shape=[jax.ShapeDtypeStruct((N_ROWS, Q_W), BF16),
                   jax.ShapeDtypeStruct((N_ROWS, KV_W), F32),
                   jax.ShapeDtypeStruct((N_ROWS, KV_W), F32)],
        compiler_params=_params(1),
        name="qkv_rope",
    )(h, w_qkv, cos, s_up, s_dn)


def _glu_kernel(h_ref, w1_ref, w2_ref, u_ref):
    h = h_ref[...]
    a = jnp.dot(h, w1_ref[...], preferred_element_type=F32)
    b = jnp.dot(h, w2_ref[...], preferred_element_type=F32)
    u_ref[...] = a * jax.nn.sigmoid(b)


def _glu(h, w_a1, w_a2):
    tn = 512
    return pl.pallas_call(
        _glu_kernel,
        grid=(CONV_CH // tn, N_ROWS // TM),
        in_specs=[pl.BlockSpec((TM, D_MODEL), lambda j, i: (i, 0)),
                  pl.BlockSpec((D_MODEL, tn), lambda j, i: (0, j)),
                  pl.BlockSpec((D_MODEL, tn), lambda j, i: (0, j))],
        out_specs=pl.BlockSpec((TM, tn), lambda j, i: (i, j)),
        out_shape=jax.ShapeDtypeStruct((N_ROWS, CONV_CH), F32),
        compiler_params=_params(2),
        name="glu_proj",
    )(h, w_a1, w_a2)


def _gates_kernel(h_ref, w_ref, s_ref):
    s_ref[...] = jax.nn.sigmoid(jnp.dot(h_ref[...], w_ref[...], preferred_element_type=F32))


def _gates(h, w_g):
    tn = 1024
    return pl.pallas_call(
        _gates_kernel,
        grid=(2 * D_MODEL // tn, N_ROWS // TM),
        in_specs=[pl.BlockSpec((TM, D_MODEL), lambda j, i: (i, 0)),
                  pl.BlockSpec((D_MODEL, tn), lambda j, i: (0, j))],
        out_specs=pl.BlockSpec((TM, tn), lambda j, i: (i, j)),
        out_shape=jax.ShapeDtypeStruct((N_ROWS, 2 * D_MODEL), F32),
        compiler_params=_params(2),
        name="gate_proj",
    )(h, w_g)


def _sink_softmax_pv(s, valid, sink_col, vh):
    s = jnp.where(valid, s, NEG)
    m = jnp.maximum(jnp.max(s, axis=-1, keepdims=True), sink_col)
    e = jnp.exp(s - m)
    denom = jnp.sum(e, axis=-1, keepdims=True) + jnp.exp(sink_col - m)
    p = (e * (1.0 / denom)).astype(BF16)
    return jnp.dot(p, vh, preferred_element_type=F32)


def _attn_prompt_kernel(sink_ref, q_ref, kp_ref, kc_ref, vp_ref, vc_ref, o_ref):
    n = pl.program_id(1)
    k2 = jnp.concatenate([kp_ref[...], kc_ref[...]], axis=0).astype(BF16)
    v2 = jnp.concatenate([vp_ref[...], vc_ref[...]], axis=0).astype(BF16)
    rows = GROUP * BLK
    t = lax.broadcasted_iota(jnp.int32, (rows, 2 * BLK), 0) & (BLK - 1)
    j = lax.broadcasted_iota(jnp.int32, (rows, 2 * BLK), 1)
    valid = (j >= t) & (j <= t + WINDOW) & ((j >= BLK) | (n > 0))
    for h in range(N_KV_HEADS):
        kh = k2[:, h * HEAD_DIM:(h + 1) * HEAD_DIM]
        vh = v2[:, h * HEAD_DIM:(h + 1) * HEAD_DIM]
        heads = [h * GROUP + g for g in range(GROUP)]
        qg = jnp.concatenate([q_ref[:, hh * HEAD_DIM:(hh + 1) * HEAD_DIM] for hh in heads], axis=0)
        s = lax.dot_general(qg, kh, (((1,), (1,)), ((), ())), preferred_element_type=F32)
        sink_col = jnp.concatenate([jnp.full((BLK, 1), sink_ref[hh], F32) for hh in heads], axis=0)
        o = _sink_softmax_pv(s, valid, sink_col, vh)
        for g, hh in enumerate(heads):
            o_ref[:, hh * HEAD_DIM:(hh + 1) * HEAD_DIM] = o[g * BLK:(g + 1) * BLK].astype(o_ref.dtype)


def _attn_prompt(sinks, q, k, v):
    cur = lambda b, n: (b * NB + n, 0)
    prev = lambda b, n: (b * NB + jnp.maximum(n - 1, 0), 0)
    return pl.pallas_call(
        _attn_prompt_kernel,
        grid=(BATCH, NB),
        in_specs=[pl.BlockSpec(memory_space=pltpu.SMEM),
                  pl.BlockSpec((BLK, Q_W), cur),
                  pl.BlockSpec((BLK, KV_W), prev),
                  pl.BlockSpec((BLK, KV_W), cur),
                  pl.BlockSpec((BLK, KV_W), prev),
                  pl.BlockSpec((BLK, KV_W), cur)],
        out_specs=pl.BlockSpec((BLK, Q_W), cur),
        out_shape=jax.ShapeDtypeStruct((N_ROWS, Q_W), BF16),
        compiler_params=_params(2),
        name="attn_prompt",
    )(sinks, q, k, k, v, v)


def _attn_sample_kernel(sink_ref, q_ref, kn_ref, vn_ref, ck_ref, cv_ref, o_in_ref, o_ref, nk_ref, nv_ref,
                        of_ref):
    del o_in_ref
    rows = GROUP * DEC_SEQ
    keys = WINDOW + DEC_SEQ
    t = lax.broadcasted_iota(jnp.int32, (rows, keys), 0) & (DEC_SEQ - 1)
    j = lax.broadcasted_iota(jnp.int32, (rows, keys), 1)
    valid = ((j < WINDOW) & (j >= t)) | ((j >= WINDOW) & (j - WINDOW <= t))
    qf = q_ref[...].astype(F32)
    for s_i in range(SEQ_PER_STEP):
        r0 = s_i * DEC_SEQ
        kk = jnp.concatenate([ck_ref[s_i], kn_ref[r0:r0 + DEC_SEQ, :]], axis=0)
        vv = jnp.concatenate([cv_ref[s_i], vn_ref[r0:r0 + DEC_SEQ, :]], axis=0)
        nk_ref[s_i] = kk[DEC_SEQ:]
        nv_ref[s_i] = vv[DEC_SEQ:]
        kb = kk.astype(BF16)
        vb = vv.astype(BF16)
        for h in range(N_KV_HEADS):
            kh = kb[:, h * HEAD_DIM:(h + 1) * HEAD_DIM]
            vh = vb[:, h * HEAD_DIM:(h + 1) * HEAD_DIM]
            heads = [h * GROUP + g for g in range(GROUP)]
            qg = jnp.concatenate([qf[r0:r0 + DEC_SEQ, hh * HEAD_DIM:(hh + 1) * HEAD_DIM]
                                  for hh in heads], axis=0).astype(BF16)
            s = lax.dot_general(qg, kh, (((1,), (1,)), ((), ())), preferred_element_type=F32)
            sink_col = jnp.concatenate([jnp.full((DEC_SEQ, 1), sink_ref[hh], F32) for hh in heads], axis=0)
            o = _sink_softmax_pv(s, valid, sink_col, vh)
            for g, hh in enumerate(heads):
                of_ref[r0:r0 + DEC_SEQ, hh * HEAD_DIM:(hh + 1) * HEAD_DIM] = o[g * DEC_SEQ:(g + 1) * DEC_SEQ]
    o_ref[...] = of_ref[...].astype(o_ref.dtype)


def _attn_sample(sinks, q, k, v, cache_k, cache_v, o_all):
    rows = SEQ_PER_STEP * DEC_SEQ
    base = N_PROMPT // rows
    row = lambda i: (base + i, 0)
    seq = lambda i: (i, 0, 0)
    cache_shape = jax.ShapeDtypeStruct((DEC_BATCH, WINDOW, KV_W), F32)
    return pl.pallas_call(
        _attn_sample_kernel,
        grid=(DEC_BATCH // SEQ_PER_STEP,),
        in_specs=[pl.BlockSpec(memory_space=pltpu.SMEM),
                  pl.BlockSpec((rows, Q_W), row),
                  pl.BlockSpec((rows, KV_W), row),
                  pl.BlockSpec((rows, KV_W), row),
                  pl.BlockSpec((SEQ_PER_STEP, WINDOW, KV_W), seq),
                  pl.BlockSpec((SEQ_PER_STEP, WINDOW, KV_W), seq),
                  pl.BlockSpec(memory_space=pl.ANY)],
        out_specs=[pl.BlockSpec((rows, Q_W), row),
                   pl.BlockSpec((SEQ_PER_STEP, WINDOW, KV_W), seq),
                   pl.BlockSpec((SEQ_PER_STEP, WINDOW, KV_W), seq)],
        out_shape=[jax.ShapeDtypeStruct((N_ROWS, Q_W), BF16), cache_shape, cache_shape],
        scratch_shapes=[pltpu.VMEM((rows, Q_W), F32)],
        input_output_aliases={6: 0},
        compiler_params=_params(1),
        name="attn_sample",
    )(sinks, q, k, v, cache_k, cache_v, o_all)


def _conv_ln_swish(ext_ref, y_ref, n_rows, wdw_ref, bdw_ref, g_ref, b_ref):
    for c in range(CONV_CH // LANES):
        sl = slice(c * LANES, (c + 1) * LANES)
        acc = jnp.zeros((n_rows, LANES), F32)
        for w in range(CONV_W):
            acc = acc + ext_ref[CONV_OFF + w:CONV_OFF + w + n_rows, sl] * wdw_ref[w:w + 1, sl]
        y_ref[0:n_rows, sl] = acc + bdw_ref[:, sl]
    y = y_ref[0:n_rows, :]
    mu = jnp.mean(y, axis=-1, keepdims=True)
    d = y - mu
    var = jnp.mean(d * d, axis=-1, keepdims=True)
    z = d * lax.rsqrt(var + EPS) * g_ref[...] + b_ref[...]
    return z * jax.nn.sigmoid(z)


def _conv_prompt_kernel(up_ref, uc_ref, wdw_ref, bdw_ref, g_ref, b_ref, c_ref, ext_ref, y_ref):
    n = pl.program_id(1)
    hist = up_ref[BLK - CONV_HIST:, :]
    ext_ref[0:CONV_HIST, :] = jnp.where(n > 0, hist, 0.0)
    ext_ref[CONV_HIST:, :] = uc_ref[...]
    c_ref[...] = _conv_ln_swish(ext_ref, y_ref, BLK, wdw_ref, bdw_ref, g_ref, b_ref).astype(c_ref.dtype)


def _conv_prompt(u, wdw, bdw, ln_g, ln_b):
    cur = lambda b, n: (b * NB + n, 0)
    prev = lambda b, n: (b * NB + jnp.maximum(n - 1, 0), 0)
    fixed = lambda b, n: (0, 0)
    return pl.pallas_call(
        _conv_prompt_kernel,
        grid=(BATCH, NB),
        in_specs=[pl.BlockSpec((BLK, CONV_CH), prev),
                  pl.BlockSpec((BLK, CONV_CH), cur),
                  pl.BlockSpec((CONV_W, CONV_CH), fixed),
                  pl.BlockSpec((1, CONV_CH), fixed),
                  pl.BlockSpec((1, CONV_CH), fixed),
                  pl.BlockSpec((1, CONV_CH), fixed)],
        out_specs=pl.BlockSpec((BLK, CONV_CH), cur),
        out_shape=jax.ShapeDtypeStruct((N_ROWS, CONV_CH), BF16),
        scratch_shapes=[pltpu.VMEM((CONV_HIST + BLK, CONV_CH), F32),
                        pltpu.VMEM((BLK, CONV_CH), F32)],
        compiler_params=_params(2),
        name="conv_prompt",
    )(u, u, wdw, bdw, ln_g, ln_b)


def _conv_sample_kernel(st_ref, u_ref, wdw_ref, bdw_ref, g_ref, b_ref, c_in_ref, c_ref, ext_ref, y_ref,
                        cf_ref):
    del c_in_ref
    for s_i in range(SEQ_PER_STEP):
        r0 = s_i * DEC_SEQ
        ext_ref[0:CONV_HIST, :] = st_ref[s_i]
        ext_ref[CONV_HIST:, :] = u_ref[r0:r0 + DEC_SEQ, :]
        cf_ref[r0:r0 + DEC_SEQ, :] = _conv_ln_swish(ext_ref, y_ref, DEC_SEQ, wdw_ref, bdw_ref, g_ref, b_ref)
    c_ref[...] = cf_ref[...].astype(c_ref.dtype)


def _conv_sample(u, state_pad, wdw, bdw, ln_g, ln_b, c_all):
    rows = SEQ_PER_STEP * DEC_SEQ
    base = N_PROMPT // rows
    row = lambda i: (base + i, 0)
    fixed = lambda i: (0, 0)
    return pl.pallas_call(
        _conv_sample_kernel,
        grid=(DEC_BATCH // SEQ_PER_STEP,),
        in_specs=[pl.BlockSpec((SEQ_PER_STEP, CONV_HIST, CONV_CH), lambda i: (i, 0, 0)),
                  pl.BlockSpec((rows, CONV_CH), row),
                  pl.BlockSpec((CONV_W, CONV_CH), fixed),
                  pl.BlockSpec((1, CONV_CH), fixed),
                  pl.BlockSpec((1, CONV_CH), fixed),
                  pl.BlockSpec((1, CONV_CH), fixed),
                  pl.BlockSpec(memory_space=pl.ANY)],
        out_specs=pl.BlockSpec((rows, CONV_CH), row),
        out_shape=jax.ShapeDtypeStruct((N_ROWS, CONV_CH), BF16),
        scratch_shapes=[pltpu.VMEM((CONV_HIST + DEC_SEQ, CONV_CH), F32),
                        pltpu.VMEM((DEC_SEQ, CONV_CH), F32),
                        pltpu.VMEM((rows, CONV_CH), F32)],
        input_output_aliases={6: 0},
        compiler_params=_params(1),
        name="conv_sample",
    )(state_pad, u, wdw, bdw, ln_g, ln_b, c_all)


def _merge_kernel(o_ref, c_ref, sa_ref, sb_ref, wa_ref, wc_ref, bc_ref, m_ref):
    br_a = jnp.dot(o_ref[...], wa_ref[...], preferred_element_type=F32)
    br_b = jnp.dot(c_ref[...], wc_ref[...], preferred_element_type=F32) + bc_ref[...]
    m_ref[...] = (sa_ref[...] * br_a + sb_ref[...] * br_b).astype(m_ref.dtype)


def _merge(o, c, sg, w_attn_out, w_conv_out, b_conv_out):
    tn = 1024
    nj = D_MODEL // tn
    return pl.pallas_call(
        _merge_kernel,
        grid=(nj, N_ROWS // TM),
        in_specs=[pl.BlockSpec((TM, Q_W), lambda j, i: (i, 0)),
                  pl.BlockSpec((TM, CONV_CH), lambda j, i: (i, 0)),
                  pl.BlockSpec((TM, tn), lambda j, i: (i, j)),
                  pl.BlockSpec((TM, tn), lambda j, i: (i, j + nj)),
                  pl.BlockSpec((Q_W, tn), lambda j, i: (0, j)),
                  pl.BlockSpec((CONV_CH, tn), lambda j, i: (0, j)),
                  pl.BlockSpec((1, tn), lambda j, i: (0, j))],
        out_specs=pl.BlockSpec((TM, tn), lambda j, i: (i, j)),
        out_shape=jax.ShapeDtypeStruct((N_ROWS, D_MODEL), BF16),
        compiler_params=_params(2),
        name="branch_merge",
    )(o, c, sg, sg, w_attn_out, w_conv_out, b_conv_out)


def _proj_res_norm_kernel(a_ref, x_ref, w_ref, g_ref, xo_ref, h_ref):
    x = x_ref[...] + jnp.dot(a_ref[...], w_ref[...], preferred_element_type=F32)
    xo_ref[...] = x
    h_ref[...] = _rms(x, g_ref[...]).astype(h_ref.dtype)


def _proj_res_norm(a, x, w, g, norm_dtype, name):
    k_dim = a.shape[1]
    row = lambda i: (i, 0)
    return pl.pallas_call(
        _proj_res_norm_kernel,
        grid=(N_ROWS // TM_SMALL,),
        in_specs=[pl.BlockSpec((TM_SMALL, k_dim), row),
                  pl.BlockSpec((TM_SMALL, D_MODEL), row),
                  _resident((k_dim, D_MODEL), lambda i: (0, 0)),
                  pl.BlockSpec((1, D_MODEL), lambda i: (0, 0))],
        out_specs=[pl.BlockSpec((TM_SMALL, D_MODEL), row),
                   pl.BlockSpec((TM_SMALL, D_MODEL), row)],
        out_shape=[jax.ShapeDtypeStruct((N_ROWS, D_MODEL), F32),
                   jax.ShapeDtypeStruct((N_ROWS, D_MODEL), norm_dtype)],
        compiler_params=_params(1),
        name=name,
    )(a, x, w, g)


def _ffn_in_kernel(h_ref, w1_ref, w3_ref, a_ref):
    h = h_ref[...]
    gate = jnp.dot(h, w1_ref[...], preferred_element_type=F32)
    up = jnp.dot(h, w3_ref[...], preferred_element_type=F32)
    a_ref[...] = (gate * jax.nn.sigmoid(gate) * up).astype(a_ref.dtype)


def _ffn_in(h, w_ffn_in):
    tn = 512
    nj = D_FF // tn
    return pl.pallas_call(
        _ffn_in_kernel,
        grid=(nj, N_ROWS // TM),
        in_specs=[pl.BlockSpec((TM, D_MODEL), lambda j, i: (i, 0)),
                  pl.BlockSpec((D_MODEL, tn), lambda j, i: (0, j)),
                  pl.BlockSpec((D_MODEL, tn), lambda j, i: (0, j + nj))],
        out_specs=pl.BlockSpec((TM, tn), lambda j, i: (i, j)),
        out_shape=jax.ShapeDtypeStruct((N_ROWS, D_FF), BF16),
        compiler_params=_params(2),
        name="ffn_in",
    )(h, w_ffn_in, w_ffn_in)


def _rope_tables(pos):
    inv = ROPE_THETA ** (-jnp.arange(ROPE_HALF, dtype=F32) * 2.0 / ROPE_DIM)
    ang = pos.astype(F32)[:, None] * inv[None, :]
    cos = jnp.cos(ang)
    sin = jnp.sin(ang)
    n = pos.shape[0]
    rest = HEAD_DIM - ROPE_DIM
    c = jnp.concatenate([cos, cos, jnp.ones((n, rest), F32)], axis=1)
    s_up = jnp.concatenate([-sin, jnp.zeros((n, HEAD_DIM - ROPE_HALF), F32)], axis=1)
    s_dn = jnp.concatenate([jnp.zeros((n, ROPE_HALF), F32), sin, jnp.zeros((n, rest), F32)], axis=1)
    rep = LANES // HEAD_DIM
    return tuple(jnp.tile(a, (1, rep)) for a in (c, s_up, s_dn))


def kernel(x_prompt, x_sample, cache_k, cache_v, state_conv, meta_tokens, w_in, attn_sinks, w_attn_out,
           conv_dw, conv_dw_bias, conv_ln_g, conv_ln_b, w_conv_out, b_conv_out, w_o, norm_mix, norm_ffn,
           w_ffn_in, w_ffn_out, norm_final):
    meta = jnp.broadcast_to(meta_tokens[None], (BATCH, N_META, D_MODEL))
    tail = jnp.zeros((BATCH, L_PAD - L_REAL, D_MODEL), F32)
    xp = jnp.concatenate([meta, x_prompt, tail], axis=1).reshape(N_PROMPT, D_MODEL)
    x = jnp.concatenate([xp, x_sample.reshape(N_SAMPLE, D_MODEL)], axis=0)

    pos_p = jnp.tile(jnp.arange(L_PAD, dtype=jnp.int32), BATCH)
    pos_s = jnp.tile(PAST_LEN + jnp.arange(DEC_SEQ, dtype=jnp.int32), DEC_BATCH)
    cos, s_up, s_dn = _rope_tables(jnp.concatenate([pos_p, pos_s]))

    w_in_b = w_in.astype(BF16)
    w_qkv = w_in_b[:, :, :QKV_W]
    w_a1 = w_in_b[:, :, QKV_W:QKV_W + CONV_CH]
    w_a2 = w_in_b[:, :, QKV_W + CONV_CH:QKV_W + 2 * CONV_CH]
    w_g = w_in_b[:, :, QKV_W + 2 * CONV_CH:]
    w_ao_b = w_attn_out.astype(BF16)
    w_co_b = w_conv_out.astype(BF16)
    w_o_b = w_o.astype(BF16)
    w_fi_b = w_ffn_in.astype(BF16)
    w_fo_b = w_ffn_out.astype(BF16)
    ck = cache_k.reshape(DEPTH, DEC_BATCH, WINDOW, KV_W)
    cv = cache_v.reshape(DEPTH, DEC_BATCH, WINDOW, KV_W)
    state_pad = jnp.pad(state_conv, ((0, 0), (0, 0), (CONV_OFF, 0), (0, 0)))

    nk_p, nv_p, nc_p, nk_s, nv_s, nc_s = [], [], [], [], [], []
    h = _rmsnorm(x, norm_mix[0][None])
    y = None
    for l in range(DEPTH):
        q, k, v = _qkv(h, w_qkv[l], cos, s_up, s_dn)
        u = _glu(h, w_a1[l], w_a2[l])
        sg = _gates(h, w_g[l])

        o = _attn_prompt(attn_sinks[l], q, k, v)
        o, nk, nv = _attn_sample(attn_sinks[l], q, k, v, ck[l], cv[l], o)
        c = _conv_prompt(u, conv_dw[l], conv_dw_bias[l][None], conv_ln_g[l][None], conv_ln_b[l][None])
        c = _conv_sample(u, state_pad[l], conv_dw[l], conv_dw_bias[l][None], conv_ln_g[l][None],
                         conv_ln_b[l][None], c)

        m = _merge(o, c, sg, w_ao_b[l], w_co_b[l], b_conv_out[l][None])
        x, h2 = _proj_res_norm(m, x, w_o_b[l], norm_ffn[l][None], BF16, "wo_res_norm")
        act = _ffn_in(h2, w_fi_b[l])
        last = l == DEPTH - 1
        g_next = norm_final if last else norm_mix[l + 1]
        x, h = _proj_res_norm(act, x, w_fo_b[l], g_next[None], F32 if last else BF16, "ffn_out_res_norm")
        if last:
            y = h

        kp = k[:N_PROMPT].reshape(BATCH, L_PAD, KV_W)[:, L_REAL - WINDOW:L_REAL]
        vp = v[:N_PROMPT].reshape(BATCH, L_PAD, KV_W)[:, L_REAL - WINDOW:L_REAL]
        nk_p.append(kp.reshape(BATCH, WINDOW, N_KV_HEADS, HEAD_DIM))
        nv_p.append(vp.reshape(BATCH, WINDOW, N_KV_HEADS, HEAD_DIM))
        nc_p.append(u[:N_PROMPT].reshape(BATCH, L_PAD, CONV_CH)[:, L_REAL - (CONV_W - 1):L_REAL])
        nk_s.append(nk.reshape(DEC_BATCH, WINDOW, N_KV_HEADS, HEAD_DIM))
        nv_s.append(nv.reshape(DEC_BATCH, WINDOW, N_KV_HEADS, HEAD_DIM))
        u_s = u[N_PROMPT:].reshape(DEC_BATCH, DEC_SEQ, CONV_CH)
        nc_s.append(jnp.concatenate([state_conv[l][:, DEC_SEQ:], u_s], axis=1))

    y_prompt = y[:N_PROMPT].reshape(BATCH, L_PAD, D_MODEL)[:, N_META:L_REAL]
    y_sample = y[N_PROMPT:].reshape(DEC_BATCH, DEC_SEQ, D_MODEL)
    return (y_prompt, y_sample, jnp.stack(nk_p), jnp.stack(nv_p), jnp.stack(nc_p),
            jnp.stack(nk_s), jnp.stack(nv_s), jnp.stack(nc_s))
```

```python
import numpy as np

import jax
import jax.numpy as jnp
from jax import lax
from jax.experimental import pallas as pl
from jax.experimental.pallas import tpu as pltpu

F32 = jnp.float32
BF16 = jnp.bfloat16

D_MODEL = 2048
DEPTH = 4
BATCH = 2
SEQ = 4096
DEC_BATCH = 32
DEC_SEQ = 8
PAST_LEN = 16384
N_META = 16
N_HEADS = 16
N_KV_HEADS = 4
HEAD_DIM = 64
GROUP = N_HEADS // N_KV_HEADS
WINDOW = 128
ROPE_DIM = 16
ROPE_HALF = ROPE_DIM // 2
ROPE_THETA = 500000.0
CONV_CH = 1024
CONV_W = 31
D_FF = 5632
Q_W = N_HEADS * HEAD_DIM
KV_W = N_KV_HEADS * HEAD_DIM
QKV_W = Q_W + 2 * KV_W
IN_COLS = QKV_W + 2 * CONV_CH + 2 * D_MODEL
EPS = 1e-6
NEG = -1e30

LANES = 128
SUBLANES = 8
BLK = 128
L_REAL = N_META + SEQ
L_PAD = 4224
FRONT = L_PAD - L_REAL
NB = L_PAD // BLK
N_PROMPT = BATCH * L_PAD
N_SAMPLE = DEC_BATCH * DEC_SEQ
N_ROWS = N_PROMPT + N_SAMPLE
CONV_HIST = 32
CONV_OFF = CONV_HIST - (CONV_W - 1)
CONV_SHIFT_ROWS = 24
SEQ_PER_STEP = 8
SAMPLE_ROWS = SEQ_PER_STEP * DEC_SEQ
SAMPLE_KEYS = SEQ_PER_STEP * (WINDOW + DEC_SEQ)

TN = 512
TM_QKV = 544
TM_GLU = 1088
TM_MERGE = 544
TM_FFN = 1088
TM_WO = 512
TM_FFN_OUT = 256
VMEM_LIMIT = 56 * 1024 * 1024


def _params(n_axes, vmem=VMEM_LIMIT):
    return pltpu.CompilerParams(dimension_semantics=("arbitrary",) * n_axes, vmem_limit_bytes=vmem)


def _resident(shape, index_map):
    return pl.BlockSpec(shape, index_map, pipeline_mode=pl.Buffered(1))


def _rms(x, g):
    return x * lax.rsqrt(jnp.mean(x * x, axis=-1, keepdims=True) + EPS) * g


def _cast_weights_once(step_axis, pairs):
    @pl.when(pl.program_id(step_axis) == 0)
    def _():
        for src, dst in pairs:
            dst[...] = src[...].astype(dst.dtype)


def _rmsnorm_kernel(x_ref, g_ref, h_ref):
    h_ref[...] = _rms(x_ref[...], g_ref[...]).astype(h_ref.dtype)


def _rmsnorm(x, g):
    tm = 512
    return pl.pallas_call(
        _rmsnorm_kernel,
        grid=(N_ROWS // tm,),
        in_specs=[pl.BlockSpec((tm, D_MODEL), lambda i: (i, 0)),
                  pl.BlockSpec((1, D_MODEL), lambda i: (0, 0))],
        out_specs=pl.BlockSpec((tm, D_MODEL), lambda i: (i, 0)),
        out_shape=jax.ShapeDtypeStruct((N_ROWS, D_MODEL), BF16),
        compiler_params=_params(1),
        name="rmsnorm",
    )(x, g)


def _qkv_kernel(h_ref, w_ref, c_ref, s1_ref, s2_ref, q_ref, k_ref, v_ref, wb_ref):
    _cast_weights_once(0, [(w_ref, wb_ref)])
    u = jnp.dot(h_ref[...], wb_ref[...], preferred_element_type=F32)
    cos = c_ref[...]
    s_up = s1_ref[...]
    s_dn = s2_ref[...]
    n_rot = (Q_W + KV_W) // LANES
    for j in range(n_rot):
        xc = u[:, j * LANES:(j + 1) * LANES]
        r = (xc * cos + pltpu.roll(xc, LANES - ROPE_HALF, 1) * s_up + pltpu.roll(xc, ROPE_HALF, 1) * s_dn)
        if j < Q_W // LANES:
            q_ref[:, j * LANES:(j + 1) * LANES] = (r * (HEAD_DIM ** -0.5)).astype(q_ref.dtype)
        else:
            jj = j - Q_W // LANES
            k_ref[:, jj * LANES:(jj + 1) * LANES] = r
    v_ref[...] = u[:, Q_W + KV_W:]


def _qkv(l, h, w_in, cos, s_up, s_dn):
    tm = TM_QKV
    row = lambda i: (i, 0)
    return pl.pallas_call(
        _qkv_kernel,
        grid=(N_ROWS // tm,),
        in_specs=[pl.BlockSpec((tm, D_MODEL), row),
                  _resident((None, D_MODEL, QKV_W), lambda i: (l, 0, 0)),
                  pl.BlockSpec((tm, LANES), row),
                  pl.BlockSpec((tm, LANES), row),
                  pl.BlockSpec((tm, LANES), row)],
        out_specs=[pl.BlockSpec((tm, Q_W), row),
                   pl.BlockSpec((tm, KV_W), row),
                   pl.BlockSpec((tm, KV_W), row)],
        out_shape=[jax.ShapeDtypeStruct((N_ROWS, Q_W), BF16),
                   jax.ShapeDtypeStruct((N_ROWS, KV_W), F32),
                   jax.ShapeDtypeStruct((N_ROWS, KV_W), F32)],
        scratch_shapes=[pltpu.VMEM((D_MODEL, QKV_W), BF16)],
        compiler_params=_params(1),
        name="qkv_rope",
    )(h, w_in, cos, s_up, s_dn)


def _glu_kernel(h_ref, w1_ref, w2_ref, u_ref, w1b_ref, w2b_ref):
    _cast_weights_once(1, [(w1_ref, w1b_ref), (w2_ref, w2b_ref)])
    h = h_ref[...]
    a = jnp.dot(h, w1b_ref[...], preferred_element_type=F32)
    b = jnp.dot(h, w2b_ref[...], preferred_element_type=F32)
    u_ref[...] = a * jax.nn.sigmoid(b)


def _glu(l, h, w_in):
    tm = TM_GLU
    c1 = QKV_W // TN
    c2 = (QKV_W + CONV_CH) // TN
    return pl.pallas_call(
        _glu_kernel,
        grid=(CONV_CH // TN, N_ROWS // tm),
        in_specs=[pl.BlockSpec((tm, D_MODEL), lambda j, i: (i, 0)),
                  pl.BlockSpec((None, D_MODEL, TN), lambda j, i: (l, 0, c1 + j)),
                  pl.BlockSpec((None, D_MODEL, TN), lambda j, i: (l, 0, c2 + j))],
        out_specs=pl.BlockSpec((tm, TN), lambda j, i: (i, j)),
        out_shape=jax.ShapeDtypeStruct((N_ROWS, CONV_CH), F32),
        scratch_shapes=[pltpu.VMEM((D_MODEL, TN), BF16), pltpu.VMEM((D_MODEL, TN), BF16)],
        compiler_params=_params(2),
        name="glu_proj",
    )(h, w_in, w_in)


def _sink_softmax_t(s, sink):
    m = jnp.maximum(jnp.max(s, axis=0, keepdims=True), sink)
    e = jnp.exp(s - m)
    denom = jnp.sum(e, axis=0, keepdims=True) + jnp.exp(sink - m)
    return (e * (1.0 / denom)).astype(BF16)


def _prompt_mask_bias():
    j = np.arange(2 * BLK)[:, None]
    t = np.arange(BLK)[None, :]
    band = (j >= t) & (j <= t + WINDOW)
    out = []
    for n in range(3):
        real = ((n - 1) * BLK + j) >= FRONT
        out.append(np.where(band & real, 0.0, NEG))
    return np.stack(out).astype(np.float32)


def _attn_prompt_kernel(sink_ref, bias_ref, q_ref, kp_ref, kc_ref, vp_ref, vc_ref, o_ref):
    k2 = jnp.concatenate([kp_ref[...], kc_ref[...]], axis=0).astype(BF16)
    v2 = jnp.concatenate([vp_ref[...], vc_ref[...]], axis=0).astype(BF16)
    bias = bias_ref[...]
    for h in range(N_KV_HEADS):
        kh = k2[:, h * HEAD_DIM:(h + 1) * HEAD_DIM]
        vh = v2[:, h * HEAD_DIM:(h + 1) * HEAD_DIM]
        heads = [h * GROUP + g for g in range(GROUP)]
        qg = jnp.concatenate([q_ref[:, hh * HEAD_DIM:(hh + 1) * HEAD_DIM] for hh in heads], axis=0)
        s_t = lax.dot_general(kh, qg, (((1,), (1,)), ((), ())), preferred_element_type=F32)
        p_t = jnp.concatenate(
            [_sink_softmax_t(s_t[:, g * BLK:(g + 1) * BLK] + bias, sink_ref[hh]) for g, hh in enumerate(heads)],
            axis=1)
        o = lax.dot_general(p_t, vh, (((0,), (0,)), ((), ())), preferred_element_type=F32)
        for g, hh in enumerate(heads):
            o_ref[:, hh * HEAD_DIM:(hh + 1) * HEAD_DIM] = o[g * BLK:(g + 1) * BLK].astype(o_ref.dtype)


def _attn_prompt(sinks, bias, q, k, v):
    cur = lambda b, n: (b * NB + n, 0)
    prev = lambda b, n: (b * NB + jnp.maximum(n - 1, 0), 0)
    return pl.pallas_call(
        _attn_prompt_kernel,
        grid=(BATCH, NB),
        in_specs=[pl.BlockSpec(memory_space=pltpu.SMEM),
                  pl.BlockSpec((None, 2 * BLK, BLK), lambda b, n: (jnp.minimum(n, 2), 0, 0)),
                  pl.BlockSpec((BLK, Q_W), cur),
                  pl.BlockSpec((BLK, KV_W), prev),
                  pl.BlockSpec((BLK, KV_W), cur),
                  pl.BlockSpec((BLK, KV_W), prev),
                  pl.BlockSpec((BLK, KV_W), cur)],
        out_specs=pl.BlockSpec((BLK, Q_W), cur),
        out_shape=jax.ShapeDtypeStruct((N_ROWS, Q_W), BF16),
        compiler_params=_params(2),
        name="attn_prompt",
    )(sinks, bias, q, k, k, v, v)


def _sample_mask_bias():
    kr = np.arange(SAMPLE_KEYS)[:, None]
    qc = np.arange(GROUP * SAMPLE_ROWS)[None, :]
    n_cache = SEQ_PER_STEP * WINDOW
    is_cache = kr < n_cache
    k_seq = np.where(is_cache, kr // WINDOW, (kr - n_cache) // DEC_SEQ)
    k_idx = np.where(is_cache, kr % WINDOW, (kr - n_cache) % DEC_SEQ)
    q_seq = (qc % SAMPLE_ROWS) // DEC_SEQ
    q_t = qc % DEC_SEQ
    ok = (k_seq == q_seq) & np.where(is_cache, k_idx >= q_t, k_idx <= q_t)
    return np.where(ok, 0.0, NEG).astype(np.float32)


def _attn_sample_kernel(sink_ref, bias_ref, q_ref, kn_ref, vn_ref, ck_ref, cv_ref, o_in_ref, o_ref, nk_ref,
                        nv_ref):
    del o_in_ref
    kn = kn_ref[...]
    vn = vn_ref[...]
    for s_i in range(SEQ_PER_STEP):
        r0 = s_i * DEC_SEQ
        nk_ref[s_i] = jnp.concatenate([ck_ref[s_i, DEC_SEQ:, :], kn[r0:r0 + DEC_SEQ]], axis=0)
        nv_ref[s_i] = jnp.concatenate([cv_ref[s_i, DEC_SEQ:, :], vn[r0:r0 + DEC_SEQ]], axis=0)
    n_cache = SEQ_PER_STEP * WINDOW
    kk = jnp.concatenate([ck_ref[...].reshape(n_cache, KV_W), kn], axis=0).astype(BF16)
    vv = jnp.concatenate([cv_ref[...].reshape(n_cache, KV_W), vn], axis=0).astype(BF16)
    bias = bias_ref[...]
    lane_group = lax.broadcasted_iota(jnp.int32, (1, GROUP * SAMPLE_ROWS), 1) // SAMPLE_ROWS
    for h in range(N_KV_HEADS):
        kh = kk[:, h * HEAD_DIM:(h + 1) * HEAD_DIM]
        vh = vv[:, h * HEAD_DIM:(h + 1) * HEAD_DIM]
        heads = [h * GROUP + g for g in range(GROUP)]
        qg = jnp.concatenate([q_ref[:, hh * HEAD_DIM:(hh + 1) * HEAD_DIM] for hh in heads], axis=0)
        s_t = lax.dot_general(kh, qg, (((1,), (1,)), ((), ())), preferred_element_type=F32) + bias
        sink = jnp.zeros((1, GROUP * SAMPLE_ROWS), F32)
        for g, hh in enumerate(heads):
            sink = jnp.where(lane_group == g, sink_ref[hh], sink)
        p_t = _sink_softmax_t(s_t, sink)
        o = lax.dot_general(p_t, vh, (((0,), (0,)), ((), ())), preferred_element_type=F32)
        for g, hh in enumerate(heads):
            o_ref[:, hh * HEAD_DIM:(hh + 1) * HEAD_DIM] = (
                o[g * SAMPLE_ROWS:(g + 1) * SAMPLE_ROWS].astype(o_ref.dtype))


def _attn_sample(l, sinks, bias, q, k, v, cache_k, cache_v, o_all):
    base = N_PROMPT // SAMPLE_ROWS
    row = lambda i: (base + i, 0)
    seq_in = lambda i: (l, i, 0, 0)
    seq_out = lambda i: (i, 0, 0)
    cache_shape = jax.ShapeDtypeStruct((DEC_BATCH, WINDOW, KV_W), F32)
    return pl.pallas_call(
        _attn_sample_kernel,
        grid=(DEC_BATCH // SEQ_PER_STEP,),
        in_specs=[pl.BlockSpec(memory_space=pltpu.SMEM),
                  pl.BlockSpec((SAMPLE_KEYS, GROUP * SAMPLE_ROWS), lambda i: (0, 0)),
                  pl.BlockSpec((SAMPLE_ROWS, Q_W), row),
                  pl.BlockSpec((SAMPLE_ROWS, KV_W), row),
                  pl.BlockSpec((SAMPLE_ROWS, KV_W), row),
                  pl.BlockSpec((None, SEQ_PER_STEP, WINDOW, KV_W), seq_in),
                  pl.BlockSpec((None, SEQ_PER_STEP, WINDOW, KV_W), seq_in),
                  pl.BlockSpec(memory_space=pl.ANY)],
        out_specs=[pl.BlockSpec((SAMPLE_ROWS, Q_W), row),
                   pl.BlockSpec((SEQ_PER_STEP, WINDOW, KV_W), seq_out),
                   pl.BlockSpec((SEQ_PER_STEP, WINDOW, KV_W), seq_out)],
        out_shape=[jax.ShapeDtypeStruct((N_ROWS, Q_W), BF16), cache_shape, cache_shape],
        input_output_aliases={7: 0},
        compiler_params=_params(1),
        name="attn_sample",
    )(sinks, bias, q, k, v, cache_k, cache_v, o_all)


def _conv_ln_swish(ext_ref, sh_ref, y_ref, n_rows, wdw_ref, bdw_ref, g_ref, b_ref):
    span = n_rows + CONV_SHIFT_ROWS
    for c in range(CONV_CH // LANES):
        sl = slice(c * LANES, (c + 1) * LANES)
        for b in range(1, SUBLANES):
            sh_ref[b - 1, 0:span, sl] = ext_ref[b:b + span, sl]
        acc = jnp.zeros((n_rows, LANES), F32)
        for w in range(CONV_W):
            a, b = divmod(CONV_OFF + w, SUBLANES)
            if b == 0:
                tap = ext_ref[a * SUBLANES:a * SUBLANES + n_rows, sl]
            else:
                tap = sh_ref[b - 1, a * SUBLANES:a * SUBLANES + n_rows, sl]
            acc = acc + tap * wdw_ref[w:w + 1, sl]
        y_ref[0:n_rows, sl] = acc + bdw_ref[:, sl]
    y = y_ref[0:n_rows, :]
    mu = jnp.mean(y, axis=-1, keepdims=True)
    d = y - mu
    var = jnp.mean(d * d, axis=-1, keepdims=True)
    z = d * lax.rsqrt(var + EPS) * g_ref[...] + b_ref[...]
    return z * jax.nn.sigmoid(z)


def _conv_prompt_kernel(up_ref, uc_ref, wdw_ref, bdw_ref, g_ref, b_ref, c_ref, ext_ref, sh_ref, y_ref):
    start = pl.program_id(1) * BLK
    hist_row = lax.broadcasted_iota(jnp.int32, (CONV_HIST, CONV_CH), 0) + (start - CONV_HIST)
    ext_ref[0:CONV_HIST, :] = jnp.where(hist_row < FRONT, 0.0, up_ref[BLK - CONV_HIST:, :])
    row = lax.broadcasted_iota(jnp.int32, (BLK, CONV_CH), 0) + start
    ext_ref[CONV_HIST:, :] = jnp.where(row < FRONT, 0.0, uc_ref[...])
    c_ref[...] = _conv_ln_swish(ext_ref, sh_ref, y_ref, BLK, wdw_ref, bdw_ref, g_ref, b_ref).astype(c_ref.dtype)


def _conv_prompt(l, u, wdw, bdw, ln_g, ln_b):
    cur = lambda b, n: (b * NB + n, 0)
    prev = lambda b, n: (b * NB + jnp.maximum(n - 1, 0), 0)
    fixed = lambda b, n: (l, 0, 0)
    return pl.pallas_call(
        _conv_prompt_kernel,
        grid=(BATCH, NB),
        in_specs=[pl.BlockSpec((BLK, CONV_CH), prev),
                  pl.BlockSpec((BLK, CONV_CH), cur),
                  pl.BlockSpec((None, CONV_W, CONV_CH), fixed),
                  pl.BlockSpec((None, 1, CONV_CH), fixed),
                  pl.BlockSpec((None, 1, CONV_CH), fixed),
                  pl.BlockSpec((None, 1, CONV_CH), fixed)],
        out_specs=pl.BlockSpec((BLK, CONV_CH), cur),
        out_shape=jax.ShapeDtypeStruct((N_ROWS, CONV_CH), BF16),
        scratch_shapes=[pltpu.VMEM((CONV_HIST + BLK, CONV_CH), F32),
                        pltpu.VMEM((SUBLANES - 1, BLK + CONV_SHIFT_ROWS, CONV_CH), F32),
                        pltpu.VMEM((BLK, CONV_CH), F32)],
        compiler_params=_params(2),
        name="conv_prompt",
    )(u, u, wdw, bdw, ln_g, ln_b)


def _conv_sample_kernel(st_ref, u_ref, wdw_ref, bdw_ref, g_ref, b_ref, c_in_ref, c_ref, ext_ref, sh_ref, y_ref,
                        cf_ref):
    del c_in_ref
    for s_i in range(SEQ_PER_STEP):
        r0 = s_i * DEC_SEQ
        ext_ref[0:CONV_HIST, :] = st_ref[s_i]
        ext_ref[CONV_HIST:, :] = u_ref[r0:r0 + DEC_SEQ, :]
        cf_ref[r0:r0 + DEC_SEQ, :] = _conv_ln_swish(ext_ref, sh_ref, y_ref, DEC_SEQ, wdw_ref, bdw_ref, g_ref,
                                                    b_ref)
    c_ref[...] = cf_ref[...].astype(c_ref.dtype)


def _conv_sample(l, u, state_pad, wdw, bdw, ln_g, ln_b, c_all):
    base = N_PROMPT // SAMPLE_ROWS
    row = lambda i: (base + i, 0)
    fixed = lambda i: (l, 0, 0)
    return pl.pallas_call(
        _conv_sample_kernel,
        grid=(DEC_BATCH // SEQ_PER_STEP,),
        in_specs=[pl.BlockSpec((None, SEQ_PER_STEP, CONV_HIST, CONV_CH), lambda i: (l, i, 0, 0)),
                  pl.BlockSpec((SAMPLE_ROWS, CONV_CH), row),
                  pl.BlockSpec((None, CONV_W, CONV_CH), fixed),
                  pl.BlockSpec((None, 1, CONV_CH), fixed),
                  pl.BlockSpec((None, 1, CONV_CH), fixed),
                  pl.BlockSpec((None, 1, CONV_CH), fixed),
                  pl.BlockSpec(memory_space=pl.ANY)],
        out_specs=pl.BlockSpec((SAMPLE_ROWS, CONV_CH), row),
        out_shape=jax.ShapeDtypeStruct((N_ROWS, CONV_CH), BF16),
        scratch_shapes=[pltpu.VMEM((CONV_HIST + DEC_SEQ, CONV_CH), F32),
                        pltpu.VMEM((SUBLANES - 1, DEC_SEQ + CONV_SHIFT_ROWS, CONV_CH), F32),
                        pltpu.VMEM((DEC_SEQ, CONV_CH), F32),
                        pltpu.VMEM((SAMPLE_ROWS, CONV_CH), F32)],
        input_output_aliases={6: 0},
        compiler_params=_params(1),
        name="conv_sample",
    )(state_pad, u, wdw, bdw, ln_g, ln_b, c_all)


def _merge_kernel(h_ref, o_ref, c_ref, wga_ref, wgb_ref, wa_ref, wc_ref, bc_ref, m_ref,
                  wga_b, wgb_b, wa_b, wc_b):
    _cast_weights_once(1, [(wga_ref, wga_b), (wgb_ref, wgb_b), (wa_ref, wa_b), (wc_ref, wc_b)])
    h = h_ref[...]
    gate_a = jax.nn.sigmoid(jnp.dot(h, wga_b[...], preferred_element_type=F32))
    br_a = jnp.dot(o_ref[...], wa_b[...], preferred_element_type=F32)
    gate_b = jax.nn.sigmoid(jnp.dot(h, wgb_b[...], preferred_element_type=F32))
    br_b = jnp.dot(c_ref[...], wc_b[...], preferred_element_type=F32) + bc_ref[...]
    m_ref[...] = (gate_a * br_a + gate_b * br_b).astype(m_ref.dtype)


def _merge(l, h, o, c, w_in, w_attn_out, w_conv_out, b_conv_out):
    tm = TM_MERGE
    ca = (QKV_W + 2 * CONV_CH) // TN
    cb = (QKV_W + 2 * CONV_CH + D_MODEL) // TN
    row = lambda j, i: (i, 0)
    col = lambda j, i: (l, 0, j)
    return pl.pallas_call(
        _merge_kernel,
        grid=(D_MODEL // TN, N_ROWS // tm),
        in_specs=[pl.BlockSpec((tm, D_MODEL), row),
                  pl.BlockSpec((tm, Q_W), row),
                  pl.BlockSpec((tm, CONV_CH), row),
                  pl.BlockSpec((None, D_MODEL, TN), lambda j, i: (l, 0, ca + j)),
                  pl.BlockSpec((None, D_MODEL, TN), lambda j, i: (l, 0, cb + j)),
                  pl.BlockSpec((None, Q_W, TN), col),
                  pl.BlockSpec((None, CONV_CH, TN), col),
                  pl.BlockSpec((None, 1, TN), col)],
        out_specs=pl.BlockSpec((tm, TN), lambda j, i: (i, j)),
        out_shape=jax.ShapeDtypeStruct((N_ROWS, D_MODEL), BF16),
        scratch_shapes=[pltpu.VMEM((D_MODEL, TN), BF16), pltpu.VMEM((D_MODEL, TN), BF16),
                        pltpu.VMEM((Q_W, TN), BF16), pltpu.VMEM((CONV_CH, TN), BF16)],
        compiler_params=_params(2),
        name="gated_merge",
    )(h, o, c, w_in, w_in, w_attn_out, w_conv_out, b_conv_out)


def _proj_res_norm_kernel(a_ref, x_ref, w_ref, g_ref, xo_ref, h_ref):
    x = x_ref[...] + jnp.dot(a_ref[...], w_ref[...], preferred_element_type=F32)
    xo_ref[...] = x
    h_ref[...] = _rms(x, g_ref[...]).astype(h_ref.dtype)


def _proj_res_norm(l, a, x, w, g, norm_dtype, tm, name):
    k_dim = a.shape[1]
    row = lambda i: (i, 0)
    return pl.pallas_call(
        _proj_res_norm_kernel,
        grid=(N_ROWS // tm,),
        in_specs=[pl.BlockSpec((tm, k_dim), row),
                  pl.BlockSpec((tm, D_MODEL), row),
                  _resident((None, k_dim, D_MODEL), lambda i: (l, 0, 0)),
                  pl.BlockSpec((1, D_MODEL), lambda i: (0, 0))],
        out_specs=[pl.BlockSpec((tm, D_MODEL), row),
                   pl.BlockSpec((tm, D_MODEL), row)],
        out_shape=[jax.ShapeDtypeStruct((N_ROWS, D_MODEL), F32),
                   jax.ShapeDtypeStruct((N_ROWS, D_MODEL), norm_dtype)],
        compiler_params=_params(1),
        name=name,
    )(a, x, w, g)


def _ffn_in_kernel(h_ref, w1_ref, w3_ref, a_ref, w1b_ref, w3b_ref):
    _cast_weights_once(1, [(w1_ref, w1b_ref), (w3_ref, w3b_ref)])
    h = h_ref[...]
    gate = jnp.dot(h, w1b_ref[...], preferred_element_type=F32)
    up = jnp.dot(h, w3b_ref[...], preferred_element_type=F32)
    a_ref[...] = (gate * jax.nn.sigmoid(gate) * up).astype(a_ref.dtype)


def _ffn_in(l, h, w_ffn_in):
    tm = TM_FFN
    nj = D_FF // TN
    return pl.pallas_call(
        _ffn_in_kernel,
        grid=(nj, N_ROWS // tm),
        in_specs=[pl.BlockSpec((tm, D_MODEL), lambda j, i: (i, 0)),
                  pl.BlockSpec((None, D_MODEL, TN), lambda j, i: (l, 0, j)),
                  pl.BlockSpec((None, D_MODEL, TN), lambda j, i: (l, 0, j + nj))],
        out_specs=pl.BlockSpec((tm, TN), lambda j, i: (i, j)),
        out_shape=jax.ShapeDtypeStruct((N_ROWS, D_FF), BF16),
        scratch_shapes=[pltpu.VMEM((D_MODEL, TN), BF16), pltpu.VMEM((D_MODEL, TN), BF16)],
        compiler_params=_params(2),
        name="ffn_in",
    )(h, w_ffn_in, w_ffn_in)


def _rope_tables(pos):
    inv = ROPE_THETA ** (-jnp.arange(ROPE_HALF, dtype=F32) * 2.0 / ROPE_DIM)
    ang = pos.astype(F32)[:, None] * inv[None, :]
    cos = jnp.cos(ang)
    sin = jnp.sin(ang)
    n = pos.shape[0]
    rest = HEAD_DIM - ROPE_DIM
    c = jnp.concatenate([cos, cos, jnp.ones((n, rest), F32)], axis=1)
    s_up = jnp.concatenate([-sin, jnp.zeros((n, HEAD_DIM - ROPE_HALF), F32)], axis=1)
    s_dn = jnp.concatenate([jnp.zeros((n, ROPE_HALF), F32), sin, jnp.zeros((n, rest), F32)], axis=1)
    rep = LANES // HEAD_DIM
    return tuple(jnp.tile(a, (1, rep)) for a in (c, s_up, s_dn))


def kernel(x_prompt, x_sample, cache_k, cache_v, state_conv, meta_tokens, w_in, attn_sinks, w_attn_out,
           conv_dw, conv_dw_bias, conv_ln_g, conv_ln_b, w_conv_out, b_conv_out, w_o, norm_mix, norm_ffn,
           w_ffn_in, w_ffn_out, norm_final):
    front = jnp.zeros((BATCH, FRONT, D_MODEL), F32)
    meta = jnp.broadcast_to(meta_tokens[None], (BATCH, N_META, D_MODEL))
    x = jnp.concatenate([jnp.concatenate([front, meta, x_prompt], axis=1).reshape(N_PROMPT, D_MODEL),
                         x_sample.reshape(N_SAMPLE, D_MODEL)], axis=0)

    pos_p = jnp.tile(jnp.arange(L_PAD, dtype=jnp.int32) - FRONT, BATCH)
    pos_s = jnp.tile(PAST_LEN + jnp.arange(DEC_SEQ, dtype=jnp.int32), DEC_BATCH)
    cos, s_up, s_dn = _rope_tables(jnp.concatenate([pos_p, pos_s]))
    bias_p = jnp.asarray(_prompt_mask_bias())
    bias_s = jnp.asarray(_sample_mask_bias())

    w_o_b = w_o.astype(BF16)
    w_fo_b = w_ffn_out.astype(BF16)
    ck = cache_k.reshape(DEPTH, DEC_BATCH, WINDOW, KV_W)
    cv = cache_v.reshape(DEPTH, DEC_BATCH, WINDOW, KV_W)
    state_pad = jnp.pad(state_conv, ((0, 0), (0, 0), (CONV_OFF, 0), (0, 0)))
    vec = lambda a: a.reshape(DEPTH, 1, a.shape[-1])
    bdw, ln_g, ln_b, b_co = vec(conv_dw_bias), vec(conv_ln_g), vec(conv_ln_b), vec(b_conv_out)

    nk_p, nv_p, nc_p, nk_s, nv_s, nc_s = [], [], [], [], [], []
    h = _rmsnorm(x, norm_mix[0][None])
    y = None
    for l in range(DEPTH):
        q, k, v = _qkv(l, h, w_in, cos, s_up, s_dn)
        u = _glu(l, h, w_in)

        o = _attn_prompt(attn_sinks[l], bias_p, q, k, v)
        o, nk, nv = _attn_sample(l, attn_sinks[l], bias_s, q, k, v, ck, cv, o)
        c = _conv_prompt(l, u, conv_dw, bdw, ln_g, ln_b)
        c = _conv_sample(l, u, state_pad, conv_dw, bdw, ln_g, ln_b, c)

        m = _merge(l, h, o, c, w_in, w_attn_out, w_conv_out, b_co)
        x, h2 = _proj_res_norm(l, m, x, w_o_b, norm_ffn[l][None], BF16, TM_WO, "wo_res_norm")
        act = _ffn_in(l, h2, w_ffn_in)
        last = l == DEPTH - 1
        g_next = norm_final if last else norm_mix[l + 1]
        x, h = _proj_res_norm(l, act, x, w_fo_b, g_next[None], F32 if last else BF16, TM_FFN_OUT,
                              "ffn_out_res_norm")
        if last:
            y = h

        kp = k[:N_PROMPT].reshape(BATCH, L_PAD, KV_W)[:, L_PAD - WINDOW:]
        vp = v[:N_PROMPT].reshape(BATCH, L_PAD, KV_W)[:, L_PAD - WINDOW:]
        nk_p.append(kp.reshape(BATCH, WINDOW, N_KV_HEADS, HEAD_DIM))
        nv_p.append(vp.reshape(BATCH, WINDOW, N_KV_HEADS, HEAD_DIM))
        nc_p.append(u[:N_PROMPT].reshape(BATCH, L_PAD, CONV_CH)[:, L_PAD - (CONV_W - 1):])
        nk_s.append(nk.reshape(DEC_BATCH, WINDOW, N_KV_HEADS, HEAD_DIM))
        nv_s.append(nv.reshape(DEC_BATCH, WINDOW, N_KV_HEADS, HEAD_DIM))
        u_s = u[N_PROMPT:].reshape(DEC_BATCH, DEC_SEQ, CONV_CH)
        nc_s.append(jnp.concatenate([state_conv[l][:, DEC_SEQ:], u_s], axis=1))

    y_prompt = y[:N_PROMPT].reshape(BATCH, L_PAD, D_MODEL)[:, FRONT + N_META:]
    y_sample = y[N_PROMPT:].reshape(DEC_BATCH, DEC_SEQ, D_MODEL)
    return (y_prompt, y_sample, jnp.stack(nk_p), jnp.stack(nv_p), jnp.stack(nc_p),
            jnp.stack(nk_s), jnp.stack(nv_s), jnp.stack(nc_s))
```
